```python
import jax, jax.numpy as jnp
from jax import lax
import numpy as np

D_MODEL = 1024
BATCH = 4
SEQ = 4096
DEPTH = 4

N_EVEN = (DEPTH + 1) // 2
N_ODD = DEPTH // 2

A_HEADS = 4
A_DK = 128
A_DV = 128
A_WIDTH = A_HEADS * A_DK
B_HEADS = 4
B_DIM = 128
B_WIDTH = B_HEADS * B_DIM
AB_IN = 4 * A_WIDTH + 3 * B_WIDTH
AB_MIX = A_WIDTH + B_WIDTH
HGRN_CHUNK = 64
SB_BLOCK = 128

C_FFN = 6 * D_MODEL
C_HALF = C_FFN // 2
C_GROUPS = 8
C_GROUP_DIM = C_HALF // C_GROUPS
C_CHUNK = 128

MLP_HIDDEN = 4 * D_MODEL
EPS = 1e-6

kernel_name = "hybrid_hgrn2_stickbreak_gmlp_trunk"


def rms_norm(x, g):
    xf = x.astype(jnp.float32)
    y = xf * lax.rsqrt(jnp.mean(xf * xf, axis=-1, keepdims=True) + EPS)
    return (y * g.astype(jnp.float32)).astype(x.dtype)


def layer_norm(x, g, b):
    xf = x.astype(jnp.float32)
    mu = jnp.mean(xf, axis=-1, keepdims=True)
    xc = xf - mu
    y = xc * lax.rsqrt(jnp.mean(xc * xc, axis=-1, keepdims=True) + EPS)
    return (y * g.astype(jnp.float32) + b.astype(jnp.float32)).astype(x.dtype)


def hgrn2_mix(q, f_logit, i, g, lb, g_norm):
    dt = q.dtype
    bsz, s_len, _ = q.shape
    n_chunks = s_len // HGRN_CHUNK
    f32 = jnp.float32
    lb = lb.astype(f32)
    f = lb + (1.0 - lb) * jax.nn.sigmoid(f_logit.astype(f32))
    log_f = jnp.log(f)
    k = 1.0 - f
    qf = jax.nn.silu(q.astype(f32))
    v = i.astype(f32)

    def to_chunks(t):
        return t.reshape(bsz, n_chunks, HGRN_CHUNK, A_HEADS, -1).transpose(1, 0, 3, 2, 4)

    qc, kc, vc, lfc = to_chunks(qf), to_chunks(k), to_chunks(v), to_chunks(log_f)
    tri = jnp.tril(jnp.ones((HGRN_CHUNK, HGRN_CHUNK), dtype=bool))

    def step(state, xs):
        qj, kj, vj, lfj = xs
        b = jnp.cumsum(lfj, axis=2)
        diff = b[:, :, :, None, :] - b[:, :, None, :, :]
        decay = jnp.exp(jnp.where(tri[:, :, None], diff, -jnp.inf))
        scores = jnp.einsum('bhtd,bhtsd,bhsd->bhts', qj, decay, kj)
        o = (jnp.einsum('bhts,bhsv->bhtv', scores, vj)
             + jnp.einsum('bhtd,bhdv->bhtv', qj * jnp.exp(b), state))
        b_last = b[:, :, -1:, :]
        state = (jnp.exp(b_last[:, :, 0, :])[..., None] * state
                 + jnp.einsum('bhsd,bhsv->bhdv', kj * jnp.exp(b_last - b), vj))
        return state, o

    state0 = jnp.zeros((bsz, A_HEADS, A_DK, A_DV), f32)
    _, o = lax.scan(step, state0, (qc, kc, vc, lfc))
    o = o.transpose(1, 0, 3, 2, 4).reshape(bsz, s_len, A_HEADS, A_DV)
    o = o * lax.rsqrt(jnp.mean(o * o, axis=-1, keepdims=True) + EPS)
    o = o.reshape(bsz, s_len, A_WIDTH) * g_norm.astype(f32) * jax.nn.silu(g.astype(f32))
    return o.astype(dt)


def stick_breaking_attention(q, k, v):
    dt = q.dtype
    bsz, s_len, _ = q.shape
    n_blocks = s_len // SB_BLOCK
    scale = B_DIM ** -0.5

    def heads(t):
        return t.reshape(bsz, s_len, B_HEADS, B_DIM).transpose(0, 2, 1, 3)

    qh, kh, vh = heads(q), heads(k), heads(v)
    q_blocks = qh.reshape(bsz, B_HEADS, n_blocks, SB_BLOCK, B_DIM).transpose(2, 0, 1, 3, 4)
    key_pos = jnp.arange(s_len)

    def one_block(args):
        qi, blk = args
        z = jnp.einsum('bhqd,bhkd->bhqk', qi, kh).astype(jnp.float32) * scale
        t_pos = blk * SB_BLOCK + jnp.arange(SB_BLOCK)
        causal = key_pos[None, :] < t_pos[:, None]
        log_beta = jax.nn.log_sigmoid(z)
        log_keep = jnp.where(causal, jax.nn.log_sigmoid(-z), 0.0)
        suffix = lax.cumsum(log_keep, axis=3, reverse=True) - log_keep
        attn = jnp.where(causal, jnp.exp(log_beta + suffix), 0.0)
        return jnp.einsum('bhqk,bhkd->bhqd', attn.astype(vh.dtype), vh)

    o = lax.map(one_block, (q_blocks, jnp.arange(n_blocks)))
    o = o.transpose(1, 0, 3, 2, 4).reshape(bsz, s_len, B_WIDTH)
    return o.astype(dt)


def chunked_gmlp(h, ln_g, ln_b, ws, bs):
    bsz, s_len, _ = h.shape
    z = jax.nn.gelu(h, approximate=False)
    u, v = z[..., :C_HALF], z[..., C_HALF:]
    v = layer_norm(v, ln_g, ln_b)
    v = v.reshape(bsz, s_len // C_CHUNK, C_CHUNK, C_GROUPS, C_GROUP_DIM)
    w = ws * jnp.tril(jnp.ones((C_CHUNK, C_CHUNK), dtype=ws.dtype))[None]
    mixed = jnp.einsum('gts,bnsgc->bntgc', w, v) + bs.T[None, None, :, :, None]
    return u * mixed.reshape(bsz, s_len, C_HALF)


def setup_inputs(seed: int = 0) -> dict:
    key = jax.random.key(seed)
    ks = jax.random.split(key, 16)
    nrm = jax.random.normal
    f32 = jnp.float32
    return {
        "x": nrm(ks[0], (BATCH, SEQ, D_MODEL), f32),
        "mix_norm": 1.0 + 0.02 * nrm(ks[1], (DEPTH, D_MODEL), f32),
        "mlp_norm": 1.0 + 0.02 * nrm(ks[2], (DEPTH, D_MODEL), f32),
        "mlp_w1": nrm(ks[3], (DEPTH, D_MODEL, MLP_HIDDEN), f32) * D_MODEL ** -0.5,
        "mlp_w2": nrm(ks[4], (DEPTH, MLP_HIDDEN, D_MODEL), f32) * MLP_HIDDEN ** -0.5,
        "ab_w_in": nrm(ks[5], (N_EVEN, D_MODEL, AB_IN), f32) * D_MODEL ** -0.5,
        "ab_w_out": nrm(ks[6], (N_EVEN, AB_MIX, D_MODEL), f32) * AB_MIX ** -0.5,
        "hgrn_lb_logits": nrm(ks[7], (N_EVEN, A_WIDTH), f32),
        "hgrn_out_norm": 1.0 + 0.02 * nrm(ks[8], (N_EVEN, A_WIDTH), f32),
        "gm_w_in": nrm(ks[9], (N_ODD, D_MODEL, C_FFN), f32) * D_MODEL ** -0.5,
        "gm_ln_g": 1.0 + 0.02 * nrm(ks[10], (N_ODD, C_HALF), f32),
        "gm_ln_b": 0.02 * nrm(ks[11], (N_ODD, C_HALF), f32),
        "gm_ws": nrm(ks[12], (N_ODD, C_GROUPS, C_CHUNK, C_CHUNK), f32) * C_CHUNK ** -0.5,
        "gm_bs": 1.0 + 0.02 * nrm(ks[13], (N_ODD, C_GROUPS, C_CHUNK), f32),
        "gm_w_out": nrm(ks[14], (N_ODD, C_HALF, D_MODEL), f32) * C_HALF ** -0.5,
        "final_norm": 1.0 + 0.02 * nrm(ks[15], (D_MODEL,), f32),
    }


def reference(x, mix_norm, mlp_norm, mlp_w1, mlp_w2, ab_w_in, ab_w_out,
              hgrn_lb_logits, hgrn_out_norm, gm_w_in, gm_ln_g, gm_ln_b, gm_ws,
              gm_bs, gm_w_out, final_norm):
    lb_cum = jnp.cumsum(jax.nn.softmax(hgrn_lb_logits.astype(jnp.float32), axis=0), axis=0)
    lower_bounds = lb_cum - lb_cum[0:1]
    splits = [int(s) for s in np.cumsum([A_WIDTH] * 4 + [B_WIDTH] * 3)[:-1]]

    for layer in range(DEPTH):
        h = rms_norm(x, mix_norm[layer])
        if layer % 2 == 0:
            e = layer // 2
            proj = h @ ab_w_in[e]
            qa, fa, ia, ga, qb, kb, vb = jnp.split(proj, splits, axis=-1)
            oa = hgrn2_mix(qa, fa, ia, ga, lower_bounds[e], hgrn_out_norm[e])
            ob = stick_breaking_attention(qb, kb, vb)
            mix = jnp.concatenate([oa, ob], axis=-1) @ ab_w_out[e]
        else:
            o = layer // 2
            gated = chunked_gmlp(h @ gm_w_in[o], gm_ln_g[o], gm_ln_b[o], gm_ws[o], gm_bs[o])
            mix = gated @ gm_w_out[o]
        x = x + mix
        h = rms_norm(x, mlp_norm[layer])
        x = x + jnp.square(jax.nn.relu(h @ mlp_w1[layer])) @ mlp_w2[layer]

    return rms_norm(x, final_norm)
```

```python
import functools

import numpy as np
import jax
import jax.numpy as jnp
from jax import lax
from jax.experimental import pallas as pl
from jax.experimental.pallas import tpu as pltpu

F32 = jnp.float32
BF16 = jnp.bfloat16

EPS = 1e-6
LANES = 128
HEAD_DIM = 128
N_HEADS = 4
HGRN_CHUNK = 128
SB_BLOCK = 128
GM_CHUNK = 128
GM_GROUPS = 8
V7X_VMEM_LIMIT = 56 * 1024 * 1024


def _params(*sem):
    return pltpu.CompilerParams(dimension_semantics=sem,
                                vmem_limit_bytes=V7X_VMEM_LIMIT)


def _resident(shape):
    return pl.BlockSpec(shape, lambda *_: (0,) * len(shape),
                        pipeline_mode=pl.Buffered(1))


def _rms(x, g):
    return x * lax.rsqrt(jnp.mean(x * x, axis=-1, keepdims=True) + EPS) * g


def _dot(a, b):
    return jnp.dot(a, b, preferred_element_type=F32)


def _dot_nt(a, b):
    return lax.dot_general(a, b, (((1,), (1,)), ((), ())),
                           preferred_element_type=F32)


def _dot_tn(a, b):
    return lax.dot_general(a, b, (((0,), (0,)), ((), ())),
                           preferred_element_type=F32)


def _split3(x):
    hi = x.astype(BF16)
    r1 = x - hi.astype(F32)
    mid = r1.astype(BF16)
    lo = (r1 - mid.astype(F32)).astype(BF16)
    return hi, mid, lo


def _silu(x):
    return x * jax.nn.sigmoid(x)


def _gelu(x):
    return 0.5 * x * (1.0 + lax.erf(x * (2.0 ** -0.5)))


def _ab_in_kernel(x_ref, g_ref, w_ref, oa_ref, ob_ref, *, n_a, n_cols, chunk):
    hn = _rms(x_ref[...], g_ref[...]).astype(BF16)
    for c0 in range(0, n_cols, chunk):
        r = _dot(hn, w_ref[:, c0:c0 + chunk])
        if c0 < n_a:
            oa_ref[:, c0:c0 + chunk] = r
        else:
            ob_ref[:, c0 - n_a:c0 - n_a + chunk] = r.astype(BF16)


def _ab_in(x, g, w, n_a, tm=512, chunk=512):
    t, d = x.shape
    n_cols = w.shape[1]
    n_b = n_cols - n_a
    return pl.pallas_call(
        functools.partial(_ab_in_kernel, n_a=n_a, n_cols=n_cols, chunk=chunk),
        grid=(t // tm,),
        in_specs=[pl.BlockSpec((tm, d), lambda i: (i, 0)),
                  _resident((1, d)),
                  _resident((d, n_cols))],
        out_specs=[pl.BlockSpec((tm, n_a), lambda i: (i, 0)),
                   pl.BlockSpec((tm, n_b), lambda i: (i, 0))],
        out_shape=[jax.ShapeDtypeStruct((t, n_a), F32),
                   jax.ShapeDtypeStruct((t, n_b), BF16)],
        compiler_params=_params("parallel"),
        name="ab_in_proj",
    )(x, g, w)


def _hgrn_levels(chunk):
    return [chunk >> (i + 1) for i in range(int(np.log2(chunk)))]


def _hgrn_sum_matrix(chunk):
    t = np.arange(chunk)[:, None]
    j = np.arange(chunk)[None, :]
    blocks = [(j <= t), (j > t)]
    for m in _hgrn_levels(chunk):
        mid = (t // (2 * m)) * (2 * m) + m - 1
        upper = (t // m) % 2 == 1
        blocks.append(np.where(upper, (j > mid) & (j <= t), (j > t) & (j <= mid)))
    return np.concatenate(blocks, axis=0).astype(np.float32)


def _hgrn_kernel(q_ref, f_ref, i_ref, g_ref, lbl_ref, gn_ref, msum_ref, o_ref,
                 state_ref, *, layer_e, chunk):
    @pl.when(pl.program_id(1) == 0)
    def _():
        state_ref[...] = jnp.zeros_like(state_ref)

    n_rows = lbl_ref.shape[0]
    rows = [lbl_ref[r:r + 1, :] for r in range(n_rows)]
    mx = functools.reduce(jnp.maximum, rows)
    ex = [jnp.exp(r - mx) for r in rows]
    den = functools.reduce(lambda a, b: a + b, ex)
    lb = jnp.zeros_like(mx)
    for r in range(1, layer_e + 1):
        lb = lb + ex[r] / den

    f = lb + (1.0 - lb) * jax.nn.sigmoid(f_ref[...])
    lf = jnp.log(f)
    k = 1.0 - f
    qf = _silu(q_ref[...])
    v = i_ref[...]
    g = g_ref[...]
    gn = gn_ref[...]

    msum = msum_ref[...]
    hi, mid, lo = _split3(lf)
    p_all = jnp.exp(_dot(msum, hi) + _dot(msum, mid) + _dot(msum, lo))
    p_b = p_all[0:chunk]
    p_d = p_all[chunk:2 * chunk]
    p_last = p_b[chunk - 1:chunk, :]

    row = lax.broadcasted_iota(jnp.int32, (chunk, chunk), 0)
    col = lax.broadcasted_iota(jnp.int32, (chunk, chunk), 1)
    row1 = lax.broadcasted_iota(jnp.int32, (chunk, 1), 0)
    levels = _hgrn_levels(chunk)

    for h in range(N_HEADS):
        sl = slice(h * HEAD_DIM, (h + 1) * HEAD_DIM)
        qh, kh = qf[:, sl], k[:, sl]
        vb = v[:, sl].astype(BF16)
        scores = jnp.where(row == col, _dot_nt(qh.astype(BF16), kh.astype(BF16)), 0.0)
        for li, m in enumerate(levels):
            sh = int(np.log2(m))
            p_l = p_all[(2 + li) * chunk:(3 + li) * chunk, sl]
            upper = ((row1 >> sh) & 1) == 1
            qt = jnp.where(upper, qh * p_l, 0.0).astype(BF16)
            kt = jnp.where(upper, 0.0, kh * p_l).astype(BF16)
            same = (row >> (sh + 1)) == (col >> (sh + 1))
            scores = scores + jnp.where(same, _dot_nt(qt, kt), 0.0)
        st = state_ref[h]
        o = _dot(scores.astype(BF16), vb)
        o = o + _dot_nt((qh * p_b[:, sl]).astype(BF16), st.astype(BF16))
        kd = (kh * p_d[:, sl]).astype(BF16)
        state_ref[h] = st * p_last[:, sl] + _dot_tn(vb, kd)
        on = o * lax.rsqrt(jnp.mean(o * o, axis=-1, keepdims=True) + EPS)
        o_ref[:, sl] = (on * gn[:, sl] * _silu(g[:, sl])).astype(BF16)


def _hgrn(pa, lb_logits, g_norm, layer_e, batch, seq, chunk=HGRN_CHUNK):
    t = pa.shape[0]
    width = N_HEADS * HEAD_DIM
    n_chunks = seq // chunk
    msum = jnp.asarray(_hgrn_sum_matrix(chunk), dtype=BF16)

    def col_block(c):
        return pl.BlockSpec((chunk, width), lambda b, n, c=c: (b * n_chunks + n, c))

    return pl.pallas_call(
        functools.partial(_hgrn_kernel, layer_e=layer_e, chunk=chunk),
        grid=(batch, n_chunks),
        in_specs=[col_block(0), col_block(1), col_block(2), col_block(3),
                  _resident(lb_logits.shape),
                  _resident((1, width)),
                  _resident(msum.shape)],
        out_specs=pl.BlockSpec((chunk, width), lambda b, n: (b * n_chunks + n, 0)),
        out_shape=jax.ShapeDtypeStruct((t, width), BF16),
        scratch_shapes=[pltpu.VMEM((N_HEADS, HEAD_DIM, HEAD_DIM), F32)],
        compiler_params=_params("parallel", "arbitrary"),
        name="hgrn2_mix",
    )(pa, pa, pa, pa, lb_logits, g_norm, msum)


def _sb_sum_matrix(blk):
    j = np.arange(blk)[:, None]
    s = np.arange(blk)[None, :]
    return np.concatenate([(j > s), np.ones((blk, blk), bool)], axis=1).astype(np.float32)


def _sb_kernel(q_ref, k_ref, v_ref, w_ref, o_ref, *, blk, scale):
    i = pl.program_id(2)
    q = q_ref[...]
    w = w_ref[...]
    row = lax.broadcasted_iota(jnp.int32, (blk, blk), 0)
    col = lax.broadcasted_iota(jnp.int32, (blk, blk), 1)

    def body(jj, carry):
        acc, c = carry
        j = i - jj
        start = pl.multiple_of(j * blk, blk)
        kj = k_ref[pl.ds(start, blk), :]
        vj = v_ref[pl.ds(start, blk), :]
        z = _dot_nt(q, kj) * scale
        log_beta = jnp.minimum(z, 0.0) - jnp.log1p(jnp.exp(-jnp.abs(z)))
        causal = (col + j * blk) < (row + i * blk)
        log_keep = jnp.where(causal, log_beta - z, 0.0)
        hi = log_keep.astype(BF16)
        lo = (log_keep - hi.astype(F32)).astype(BF16)
        sums = _dot(hi, w) + _dot(lo, w)
        suffix = sums[:, :blk] + c
        attn = jnp.where(causal, jnp.exp(log_beta + suffix), 0.0)
        acc = acc + _dot(attn.astype(BF16), vj)
        return acc, c + sums[:, blk:]

    acc, _ = lax.fori_loop(
        0, i + 1, body,
        (jnp.zeros((blk, HEAD_DIM), F32), jnp.zeros((blk, blk), F32)))
    o_ref[...] = acc.astype(BF16)


def _sb_attention(pb, batch, seq, blk=SB_BLOCK):
    t = pb.shape[0]
    n_blocks = seq // blk
    w = jnp.asarray(_sb_sum_matrix(blk), dtype=BF16)
    return pl.pallas_call(
        functools.partial(_sb_kernel, blk=blk, scale=HEAD_DIM ** -0.5),
        grid=(batch, N_HEADS, n_blocks),
        in_specs=[
            pl.BlockSpec((blk, HEAD_DIM), lambda b, h, i: (b * n_blocks + i, h)),
            pl.BlockSpec((seq, HEAD_DIM), lambda b, h, i: (b, N_HEADS + h)),
            pl.BlockSpec((seq, HEAD_DIM), lambda b, h, i: (b, 2 * N_HEADS + h)),
            _resident(w.shape)],
        out_specs=pl.BlockSpec((blk, HEAD_DIM), lambda b, h, i: (b * n_blocks + i, h)),
        out_shape=jax.ShapeDtypeStruct((t, N_HEADS * HEAD_DIM), BF16),
        compiler_params=_params("parallel", "parallel", "arbitrary"),
        name="stick_breaking",
    )(pb, pb, pb, w)


def _ab_out_kernel(oa_ref, ob_ref, w_ref, x_ref, o_ref):
    wa = oa_ref.shape[1]
    o_ref[...] = (x_ref[...] + _dot(oa_ref[...], w_ref[:wa, :])
                  + _dot(ob_ref[...], w_ref[wa:, :]))


def _ab_out(oa, ob, w, x, tm=512):
    t, d = x.shape
    wa, wb = oa.shape[1], ob.shape[1]
    return pl.pallas_call(
        _ab_out_kernel,
        grid=(t // tm,),
        in_specs=[pl.BlockSpec((tm, wa), lambda i: (i, 0)),
                  pl.BlockSpec((tm, wb), lambda i: (i, 0)),
                  _resident(w.shape),
                  pl.BlockSpec((tm, d), lambda i: (i, 0))],
        out_specs=pl.BlockSpec((tm, d), lambda i: (i, 0)),
        out_shape=jax.ShapeDtypeStruct((t, d), F32),
        compiler_params=_params("parallel"),
        name="ab_out_proj",
    )(oa, ob, w, x)


def _gmlp_kernel(x_ref, g_ref, win_ref, lng_ref, lnb_ref, ws_ref, bs_ref, wout_ref,
                 o_ref, v_ref, *, tm, half, col_chunk):
    x = x_ref[...]
    hn = _rms(x, g_ref[...]).astype(BF16)

    for c0 in range(0, half, col_chunk):
        v_ref[:, c0:c0 + col_chunk] = _gelu(_dot(hn, win_ref[:, half + c0:half + c0 + col_chunk]))
    v = v_ref[...]
    xc = v - jnp.mean(v, axis=-1, keepdims=True)
    rstd = lax.rsqrt(jnp.mean(xc * xc, axis=-1, keepdims=True) + EPS)
    v_ref[...] = xc * rstd * lng_ref[...] + lnb_ref[...]

    row = lax.broadcasted_iota(jnp.int32, (GM_CHUNK, GM_CHUNK), 0)
    col = lax.broadcasted_iota(jnp.int32, (GM_CHUNK, GM_CHUNK), 1)
    gdim = half // GM_GROUPS
    acc = x
    for gi in range(GM_GROUPS):
        cols = slice(gi * gdim, (gi + 1) * gdim)
        u = _gelu(_dot(hn, win_ref[:, cols]))
        w = jnp.where(row >= col, ws_ref[gi], 0.0).astype(BF16)
        bias = bs_ref[gi]
        parts = []
        for r0 in range(0, tm, GM_CHUNK):
            mixed = _dot(w, v_ref[r0:r0 + GM_CHUNK, cols].astype(BF16)) + bias
            parts.append(u[r0:r0 + GM_CHUNK] * mixed)
        gated = jnp.concatenate(parts, axis=0).astype(BF16)
        acc = acc + _dot(gated, wout_ref[cols, :])
    o_ref[...] = acc


def _gmlp(x, g, w_in, ln_g, ln_b, ws, bs, w_out, tm=256):
    t, d = x.shape
    half = w_out.shape[0]
    return pl.pallas_call(
        functools.partial(_gmlp_kernel, tm=tm, half=half, col_chunk=512),
        grid=(t // tm,),
        in_specs=[pl.BlockSpec((tm, d), lambda i: (i, 0)),
                  _resident((1, d)),
                  _resident(w_in.shape),
                  _resident((1, half)),
                  _resident((1, half)),
                  _resident(ws.shape),
                  _resident(bs.shape),
                  _resident(w_out.shape)],
        out_specs=pl.BlockSpec((tm, d), lambda i: (i, 0)),
        out_shape=jax.ShapeDtypeStruct((t, d), F32),
        scratch_shapes=[pltpu.VMEM((tm, half), F32)],
        compiler_params=_params("parallel"),
        name="gmlp_mix",
    )(x, g, w_in, ln_g, ln_b, ws, bs, w_out)


def _mlp_kernel(x_ref, g_ref, w1_ref, w2_ref, gf_ref, o_ref, *, hidden, chunk, final):
    x = x_ref[...]
    hn = _rms(x, g_ref[...]).astype(BF16)
    acc = x
    for c0 in range(0, hidden, chunk):
        a = jnp.maximum(_dot(hn, w1_ref[:, c0:c0 + chunk]), 0.0)
        acc = acc + _dot((a * a).astype(BF16), w2_ref[c0:c0 + chunk, :])
    if final:
        acc = _rms(acc, gf_ref[...])
    o_ref[...] = acc


def _mlp(x, g, w1, w2, g_final, final, tm=512, chunk=512):
    t, d = x.shape
    hidden = w1.shape[1]
    return pl.pallas_call(
        functools.partial(_mlp_kernel, hidden=hidden, chunk=chunk, final=final),
        grid=(t // tm,),
        in_specs=[pl.BlockSpec((tm, d), lambda i: (i, 0)),
                  _resident((1, d)),
                  _resident(w1.shape),
                  _resident(w2.shape),
                  _resident((1, d))],
        out_specs=pl.BlockSpec((tm, d), lambda i: (i, 0)),
        out_shape=jax.ShapeDtypeStruct((t, d), F32),
        compiler_params=_params("parallel"),
        name="relu2_mlp",
    )(x, g, w1, w2, g_final)


def kernel(x, mix_norm, mlp_norm, mlp_w1, mlp_w2, ab_w_in, ab_w_out, hgrn_lb_logits,
           hgrn_out_norm, gm_w_in, gm_ln_g, gm_ln_b, gm_ws, gm_bs, gm_w_out, final_norm):
    batch, seq, d = x.shape
    depth = mix_norm.shape[0]
    a_width = hgrn_lb_logits.shape[1]
    n_a = 4 * a_width
    xf = x.reshape(batch * seq, d).astype(F32)
    lb_logits = hgrn_lb_logits.astype(F32)
    g_final = final_norm.reshape(1, d).astype(F32)

    for layer in range(depth):
        g_mix = mix_norm[layer].reshape(1, d).astype(F32)
        if layer % 2 == 0:
            e = layer // 2
            pa, pb = _ab_in(xf, g_mix, ab_w_in[e].astype(BF16), n_a)
            oa = _hgrn(pa, lb_logits, hgrn_out_norm[e].reshape(1, a_width).astype(F32),
                       e, batch, seq)
            ob = _sb_attention(pb, batch, seq)
            xf = _ab_out(oa, ob, ab_w_out[e].astype(BF16), xf)
        else:
            o = layer // 2
            half = gm_w_out.shape[1]
            xf = _gmlp(xf, g_mix, gm_w_in[o].astype(BF16),
                       gm_ln_g[o].reshape(1, half).astype(F32),
                       gm_ln_b[o].reshape(1, half).astype(F32),
                       gm_ws[o].astype(F32),
                       gm_bs[o].reshape(GM_GROUPS, GM_CHUNK, 1).astype(F32),
                       gm_w_out[o].astype(BF16))
        xf = _mlp(xf, mlp_norm[layer].reshape(1, d).astype(F32),
                  mlp_w1[layer].astype(BF16), mlp_w2[layer].astype(BF16),
                  g_final, final=(layer == depth - 1))
    return xf.reshape(batch, seq, d).astype(x.dtype)
```
